```python
import math
import jax, jax.numpy as jnp
from jax import lax
import numpy as np

D_MODEL = 1024
BATCH = 8
SEQ = 4096
DEPTH = 4

PLE_DIM = 256
CHUNK = 128
A_WIDTH = D_MODEL
A_GROUPS = 8
A_GROUP_DIM = A_WIDTH // A_GROUPS
B_HEAD_DIM = 64
B_HEADS = D_MODEL // (2 * B_HEAD_DIM)
B_WIDTH = B_HEADS * 2 * B_HEAD_DIM
Q_BLOCK = 128
EPS = 1e-6
NEG = -1e30
IN_WIDTHS = (A_WIDTH, A_WIDTH, A_WIDTH, B_WIDTH, B_WIDTH, B_WIDTH, B_WIDTH, D_MODEL, D_MODEL)
IN_TOTAL = sum(IN_WIDTHS)
SPLIT_POINTS = tuple(int(c) for c in np.cumsum(IN_WIDTHS)[:-1])

kernel_name = "hybrid_gmlp_diffattn_gated_block"


def rmsnorm(x, g):
    xf = x.astype(jnp.float32)
    y = xf * lax.rsqrt(jnp.mean(xf * xf, axis=-1, keepdims=True) + EPS)
    return (y * g.astype(jnp.float32)).astype(x.dtype)


def layernorm(x, g, b):
    xf = x.astype(jnp.float32)
    mu = jnp.mean(xf, axis=-1, keepdims=True)
    xc = xf - mu
    y = xc * lax.rsqrt(jnp.mean(xc * xc, axis=-1, keepdims=True) + EPS)
    return (y * g.astype(jnp.float32) + b.astype(jnp.float32)).astype(x.dtype)


def lambda_init(layer_idx):
    return 0.8 - 0.6 * math.exp(-0.3 * layer_idx)


def gmlp_branch(u, v, z, ln_g, ln_b, ws, bs):
    B, S, _ = v.shape
    n_chunks = S // CHUNK
    u = jax.nn.gelu(u, approximate=False)
    v = layernorm(jax.nn.gelu(v, approximate=False), ln_g, ln_b)
    vc = v.reshape(B, n_chunks, CHUNK, A_GROUPS, A_GROUP_DIM)
    w = jnp.tril(ws)
    y = jnp.einsum('gts,bcsgd->bctgd', w, vc) + bs.T[None, None, :, :, None]
    y = y.reshape(B, S, A_WIDTH)
    return u * y * jax.nn.silu(z)


def diff_attn_branch(q, k, v, z, lam_q, lam_k, subln_g, lam_init):
    B, S, _ = q.shape
    q = q.reshape(B, S, B_HEADS, 2, B_HEAD_DIM)
    k = k.reshape(B, S, B_HEADS, 2, B_HEAD_DIM)
    v = v.reshape(B, S, B_HEADS, 2 * B_HEAD_DIM)
    lq = lam_q.astype(jnp.float32)
    lk = lam_k.astype(jnp.float32)
    lam = jnp.exp(jnp.sum(lq[0] * lk[0])) - jnp.exp(jnp.sum(lq[1] * lk[1])) + lam_init
    scale = B_HEAD_DIM ** -0.5
    n_q = S // Q_BLOCK
    q_blocks = q.reshape(B, n_q, Q_BLOCK, B_HEADS, 2, B_HEAD_DIM).transpose(1, 0, 2, 3, 4, 5)
    k_pos = jnp.arange(S)

    def one_block(args):
        q_blk, qi = args
        s = jnp.einsum('bqhmd,bkhmd->bhmqk', q_blk, k).astype(jnp.float32) * scale
        q_pos = qi * Q_BLOCK + jnp.arange(Q_BLOCK)
        s = jnp.where((q_pos[:, None] >= k_pos[None, :])[None, None, None], s, NEG)
        pm = jax.nn.softmax(s, axis=-1)
        a = pm[:, :, 0] - lam * pm[:, :, 1]
        return jnp.einsum('bhqk,bkhe->bqhe', a.astype(v.dtype), v)

    o = lax.map(one_block, (q_blocks, jnp.arange(n_q)))
    o = o.transpose(1, 0, 2, 3, 4).reshape(B, S, B_HEADS, 2 * B_HEAD_DIM)
    o = rmsnorm(o, subln_g) * (1.0 - lam_init)
    o = o.reshape(B, S, B_WIDTH)
    return o * jax.nn.silu(z)


def setup_inputs(seed: int = 0) -> dict:
    key = jax.random.key(seed)
    ks = jax.random.split(key, 20)

    def nrm(k, shape, scale):
        return jax.random.normal(k, shape, jnp.float32) * scale

    return {
        "x": nrm(ks[0], (BATCH, SEQ, D_MODEL), 1.0),
        "p": nrm(ks[1], (DEPTH, BATCH, SEQ, PLE_DIM), 1.0),
        "norm_g": 1.0 + nrm(ks[2], (DEPTH, D_MODEL), 0.05),
        "w_in": nrm(ks[3], (DEPTH, D_MODEL, IN_TOTAL), D_MODEL ** -0.5),
        "a_ln_g": 1.0 + nrm(ks[4], (DEPTH, A_WIDTH), 0.05),
        "a_ln_b": nrm(ks[5], (DEPTH, A_WIDTH), 0.02),
        "a_ws": nrm(ks[6], (DEPTH, A_GROUPS, CHUNK, CHUNK), CHUNK ** -0.5),
        "a_bs": 1.0 + nrm(ks[7], (DEPTH, A_GROUPS, CHUNK), 0.05),
        "lam_q": nrm(ks[8], (DEPTH, 2, B_HEAD_DIM), 0.1),
        "lam_k": nrm(ks[9], (DEPTH, 2, B_HEAD_DIM), 0.1),
        "subln_g": 1.0 + nrm(ks[10], (DEPTH, 2 * B_HEAD_DIM), 0.05),
        "w_a_out": nrm(ks[11], (DEPTH, A_WIDTH, D_MODEL), A_WIDTH ** -0.5),
        "w_b_out": nrm(ks[12], (DEPTH, B_WIDTH, D_MODEL), B_WIDTH ** -0.5),
        "w_o": nrm(ks[13], (DEPTH, D_MODEL, D_MODEL), D_MODEL ** -0.5),
        "ple_norm_g": 1.0 + nrm(ks[14], (DEPTH, D_MODEL), 0.05),
        "w_ple": nrm(ks[15], (DEPTH, PLE_DIM, D_MODEL), PLE_DIM ** -0.5),
        "w_ple_gate": nrm(ks[16], (DEPTH, D_MODEL, D_MODEL), D_MODEL ** -0.5),
        "final_g": 1.0 + nrm(ks[17], (D_MODEL,), 0.05),
    }


def reference(x, p, norm_g, w_in, a_ln_g, a_ln_b, a_ws, a_bs, lam_q, lam_k, subln_g,
              w_a_out, w_b_out, w_o, ple_norm_g, w_ple, w_ple_gate, final_g):
    for i in range(DEPTH):
        h = rmsnorm(x, norm_g[i])
        proj = h @ w_in[i]
        u_a, v_a, z_a, q, k, v_b, z_b, g_a, g_b = jnp.split(proj, SPLIT_POINTS, axis=-1)
        y_a = gmlp_branch(u_a, v_a, z_a, a_ln_g[i], a_ln_b[i], a_ws[i], a_bs[i]) @ w_a_out[i]
        y_b = diff_attn_branch(q, k, v_b, z_b, lam_q[i], lam_k[i], subln_g[i],
                               lambda_init(i)) @ w_b_out[i]
        merged = jax.nn.sigmoid(g_a) * y_a + jax.nn.sigmoid(g_b) * y_b
        x = x + merged @ w_o[i]
        ple_gate = jax.nn.sigmoid(rmsnorm(x, ple_norm_g[i]) @ w_ple_gate[i])
        x = x + (p[i] @ w_ple[i]) * ple_gate
    return rmsnorm(x, final_g)
```

```python
import functools
import math

import jax
import jax.numpy as jnp
from jax import lax
from jax.experimental import pallas as pl
from jax.experimental.pallas import tpu as pltpu

EPS = 1e-6
MASK_VALUE = -1e30

D_MODEL = 1024
PLE_DIM = 256
CHUNK = 128
GROUPS = 8
GROUP_DIM = D_MODEL // GROUPS
HEAD_DIM = 64
HEAD_WIDTH = 2 * HEAD_DIM
HEADS = D_MODEL // HEAD_WIDTH
COL_U, COL_V, COL_ZA, COL_Q, COL_K, COL_VB, COL_ZB, COL_GA, COL_GB = range(9)
REST_COLS = (COL_U, COL_V, COL_ZA, COL_ZB, COL_GA, COL_GB)

QKV_TILE = 512
LAYER_TILE = 512
ATTN_Q_TILE = 256
ATTN_KV_TILE = 256
ATTN_HEADS_PER_STEP = 4
VMEM_LIMIT_BYTES = 56 * 1024 * 1024

_BF16 = jnp.bfloat16
_F32 = jnp.float32


def _dot(a, b):
    return jnp.dot(a, b, preferred_element_type=_F32)


def _rmsnorm(x, g):
    return x * lax.rsqrt(jnp.mean(x * x, axis=-1, keepdims=True) + EPS) * g


def _gelu(x):
    return 0.5 * x * (1.0 + lax.erf(x * (1.0 / math.sqrt(2.0))))


def _silu(x):
    return x * jax.nn.sigmoid(x)


def _resident(shape, index_map):
    return pl.BlockSpec(shape, index_map, pipeline_mode=pl.Buffered(1))


def _qkv_kernel(x_ref, g_ref, wq_ref, wkt_ref, wv_ref, q_ref, kt_ref, v_ref):
    h = _rmsnorm(x_ref[0], g_ref[...]).astype(_BF16)
    q_ref[0] = (_dot(h, wq_ref[...]) * (HEAD_DIM ** -0.5)).astype(_BF16)
    kt = lax.dot_general(wkt_ref[...], h, (((1,), (1,)), ((), ())),
                         preferred_element_type=_F32)
    kt_ref[0] = kt.astype(_BF16)
    v_ref[0] = _dot(h, wv_ref[...]).astype(_BF16)


def _qkv_projection(x, g, wq, wkt, wv):
    B, S, D = x.shape
    tm = min(QKV_TILE, S)
    tok = pl.BlockSpec((1, tm, D), lambda b, s: (b, s, 0))
    w = _resident((D, D), lambda b, s: (0, 0))
    return pl.pallas_call(
        _qkv_kernel,
        grid=(B, S // tm),
        in_specs=[tok, _resident((1, D), lambda b, s: (0, 0)), w, w, w],
        out_specs=[tok, pl.BlockSpec((1, D, tm), lambda b, s: (b, 0, s)), tok],
        out_shape=[jax.ShapeDtypeStruct((B, S, D), _BF16),
                   jax.ShapeDtypeStruct((B, D, S), _BF16),
                   jax.ShapeDtypeStruct((B, S, D), _BF16)],
        compiler_params=pltpu.CompilerParams(vmem_limit_bytes=VMEM_LIMIT_BYTES),
        name="qkv_projection",
    )(x, g, wq, wkt, wv)


def _attn_kernel(lq_ref, lk_ref, sg_ref, q_ref, kt_ref, v_ref, o_ref,
                 qz_scr, m_scr, l_scr, acc_scr, *, lam_init, tq, tk, heads):
    qi = pl.program_id(2)
    lane = lax.broadcasted_iota(jnp.int32, (tq, HEAD_WIDTH), 1)

    for h in range(heads):
        q = q_ref[0, :, h * HEAD_WIDTH:(h + 1) * HEAD_WIDTH]
        zero = jnp.zeros_like(q)
        qz_scr[h, :tq, :] = jnp.where(lane < HEAD_DIM, q, zero)
        qz_scr[h, tq:, :] = jnp.where(lane >= HEAD_DIM, q, zero)
    m_scr[...] = jnp.full(m_scr.shape, MASK_VALUE, _F32)
    l_scr[...] = jnp.zeros(l_scr.shape, _F32)
    acc_scr[...] = jnp.zeros(acc_scr.shape, _F32)

    def step(j, masked):
        start = pl.multiple_of(j * tk, tk)
        for h in range(heads):
            hs = slice(h * HEAD_WIDTH, (h + 1) * HEAD_WIDTH)
            s = _dot(qz_scr[h], kt_ref[0, hs, pl.ds(start, tk)])
            if masked:
                row = lax.broadcasted_iota(jnp.int32, (2 * tq, tk), 0)
                col = lax.broadcasted_iota(jnp.int32, (2 * tq, tk), 1)
                row = jnp.where(row >= tq, row - tq, row)
                s = jnp.where(col <= row, s, MASK_VALUE)
            m_prev = m_scr[h]
            m_new = jnp.maximum(m_prev, jnp.max(s, axis=1, keepdims=True))
            alpha = jnp.exp(m_prev - m_new)
            p = jnp.exp(s - m_new[:, :1])
            l_scr[h] = alpha * l_scr[h] + jnp.sum(p, axis=1, keepdims=True)
            pv = _dot(p.astype(_BF16), v_ref[0, pl.ds(start, tk), hs])
            acc_scr[h] = alpha * acc_scr[h] + pv
            m_scr[h] = m_new

    def body(j, carry):
        step(j, masked=False)
        return carry

    lax.fori_loop(0, qi, body, 0)
    step(qi, masked=True)

    prod = lq_ref[...] * lk_ref[...]
    e = jnp.exp(jnp.sum(prod, axis=1, keepdims=True))
    lam = e[0:1, :] - e[1:2, :] + lam_init
    for h in range(heads):
        o1 = acc_scr[h, :tq, :] / l_scr[h, :tq, :]
        o2 = acc_scr[h, tq:, :] / l_scr[h, tq:, :]
        o = o1 - lam * o2
        o = _rmsnorm(o, sg_ref[...]) * (1.0 - lam_init)
        o_ref[0, :, h * HEAD_WIDTH:(h + 1) * HEAD_WIDTH] = o.astype(o_ref.dtype)


def _diff_attention(q, kt, v, lam_q, lam_k, subln_g, lam_init):
    B, S, D = q.shape
    tq = tk = min(ATTN_Q_TILE, S)
    assert ATTN_Q_TILE == ATTN_KV_TILE and S % tq == 0
    heads = ATTN_HEADS_PER_STEP
    width = heads * HEAD_WIDTH
    small = lambda shape: pl.BlockSpec(shape, lambda b, g, i: (0, 0))
    kernel = functools.partial(_attn_kernel, lam_init=lam_init, tq=tq, tk=tk, heads=heads)
    return pl.pallas_call(
        kernel,
        grid=(B, D // width, S // tq),
        in_specs=[small((2, HEAD_DIM)), small((2, HEAD_DIM)), small((1, HEAD_WIDTH)),
                  pl.BlockSpec((1, tq, width), lambda b, g, i: (b, i, g)),
                  pl.BlockSpec((1, width, S), lambda b, g, i: (b, g, 0)),
                  pl.BlockSpec((1, S, width), lambda b, g, i: (b, 0, g))],
        out_specs=pl.BlockSpec((1, tq, width), lambda b, g, i: (b, i, g)),
        out_shape=jax.ShapeDtypeStruct((B, S, D), _BF16),
        scratch_shapes=[pltpu.VMEM((heads, 2 * tq, HEAD_WIDTH), _BF16),
                        pltpu.VMEM((heads, 2 * tq, HEAD_WIDTH), _F32),
                        pltpu.VMEM((heads, 2 * tq, HEAD_WIDTH), _F32),
                        pltpu.VMEM((heads, 2 * tq, HEAD_WIDTH), _F32)],
        compiler_params=pltpu.CompilerParams(vmem_limit_bytes=VMEM_LIMIT_BYTES),
        name="diff_attention",
    )(lam_q, lam_k, subln_g, q, kt, v)


def _layer_kernel(x_ref, o_ref, p_ref, ng_ref, w_ref, lng_ref, lnb_ref, ws_ref, bias_ref,
                  wa_ref, wb_ref, wo_ref, png_ref, wple_ref, wpg_ref, fg_ref, out_ref,
                  *, tm, final):
    x = x_ref[0]
    h = _rmsnorm(x, ng_ref[...]).astype(_BF16)

    def proj(col):
        k = REST_COLS.index(col)
        return _dot(h, w_ref[:, k * D_MODEL:(k + 1) * D_MODEL])

    v = _gelu(proj(COL_V))
    mu = jnp.mean(v, axis=-1, keepdims=True)
    vc = v - mu
    v = vc * lax.rsqrt(jnp.mean(vc * vc, axis=-1, keepdims=True) + EPS) * lng_ref[...] + lnb_ref[...]
    v = v.astype(_BF16)

    n_chunks = tm // CHUNK
    t_out = lax.broadcasted_iota(jnp.int32, (CHUNK, CHUNK), 0)
    t_in = lax.broadcasted_iota(jnp.int32, (CHUNK, CHUNK), 1)
    y_groups = []
    for g in range(GROUPS):
        gs = slice(g * GROUP_DIM, (g + 1) * GROUP_DIM)
        w_g = jnp.where(t_out >= t_in, ws_ref[g], 0.0).astype(_BF16)
        v_wide = jnp.concatenate([v[c * CHUNK:(c + 1) * CHUNK, gs] for c in range(n_chunks)], axis=1)
        y_wide = _dot(w_g, v_wide)
        y_groups.append([y_wide[:, c * GROUP_DIM:(c + 1) * GROUP_DIM] for c in range(n_chunks)])
    y = jnp.concatenate(
        [jnp.concatenate([y_groups[g][c] for g in range(GROUPS)], axis=1) + bias_ref[...]
         for c in range(n_chunks)], axis=0)

    t_a = (_gelu(proj(COL_U)) * y * _silu(proj(COL_ZA))).astype(_BF16)
    y_a = _dot(t_a, wa_ref[...])

    t_b = (o_ref[0].astype(_F32) * _silu(proj(COL_ZB))).astype(_BF16)
    y_b = _dot(t_b, wb_ref[...])

    merged = jax.nn.sigmoid(proj(COL_GA)) * y_a + jax.nn.sigmoid(proj(COL_GB)) * y_b
    x = x + _dot(merged.astype(_BF16), wo_ref[...])

    gate = jax.nn.sigmoid(_dot(_rmsnorm(x, png_ref[...]).astype(_BF16), wpg_ref[...]))
    x = x + _dot(p_ref[0].astype(_BF16), wple_ref[...]) * gate
    if final:
        x = _rmsnorm(x, fg_ref[...])
    out_ref[0] = x


def _fused_layer(x, o, p, ng, w_rest, lng, lnb, ws, bias, wa, wb, wo, png, wple, wpg, fg, final):
    B, S, D = x.shape
    tm = min(LAYER_TILE, S)
    tok = lambda width: pl.BlockSpec((1, tm, width), lambda b, s: (b, s, 0))
    const2 = lambda shape: _resident(shape, lambda b, s: (0, 0))
    row = const2((1, D))
    sq = const2((D, D))
    kernel = functools.partial(_layer_kernel, tm=tm, final=final)
    return pl.pallas_call(
        kernel,
        grid=(B, S // tm),
        in_specs=[tok(D), tok(D), tok(PLE_DIM), row,
                  const2(w_rest.shape), row, row,
                  _resident((GROUPS, CHUNK, CHUNK), lambda b, s: (0, 0, 0)),
                  const2((CHUNK, D)),
                  sq, sq, sq, row, const2((PLE_DIM, D)), sq, row],
        out_specs=tok(D),
        out_shape=jax.ShapeDtypeStruct((B, S, D), _F32),
        compiler_params=pltpu.CompilerParams(vmem_limit_bytes=VMEM_LIMIT_BYTES),
        name="fused_layer",
    )(x, o, p, ng, w_rest, lng, lnb, ws, bias, wa, wb, wo, png, wple, wpg, fg)


def _lambda_init(layer_idx):
    return 0.8 - 0.6 * math.exp(-0.3 * layer_idx)


def kernel(x, p, norm_g, w_in, a_ln_g, a_ln_b, a_ws, a_bs, lam_q, lam_k, subln_g,
           w_a_out, w_b_out, w_o, ple_norm_g, w_ple, w_ple_gate, final_g):
    depth = w_in.shape[0]
    D = D_MODEL
    col = lambda w, c: w[:, c * D:(c + 1) * D]
    row = lambda g: g.reshape(1, -1)
    for i in range(depth):
        w = w_in[i].astype(_BF16)
        wq, wv = col(w, COL_Q), col(w, COL_VB)
        wkt = col(w, COL_K).T
        w_rest = jnp.concatenate([col(w, c) for c in REST_COLS], axis=1)
        bias = jnp.repeat(a_bs[i].T, GROUP_DIM, axis=1)
        q, kt, v = _qkv_projection(x, row(norm_g[i]), wq, wkt, wv)
        o = _diff_attention(q, kt, v, lam_q[i], lam_k[i], row(subln_g[i]), _lambda_init(i))
        x = _fused_layer(
            x, o, p[i], row(norm_g[i]), w_rest, row(a_ln_g[i]), row(a_ln_b[i]), a_ws[i], bias,
            w_a_out[i].astype(_BF16), w_b_out[i].astype(_BF16), w_o[i].astype(_BF16),
            row(ple_norm_g[i]), w_ple[i].astype(_BF16), w_ple_gate[i].astype(_BF16),
            row(final_g), final=(i == depth - 1))
    return x
```

```python
import functools
import math

import jax
import jax.numpy as jnp
from jax import lax
from jax.experimental import pallas as pl
from jax.experimental.pallas import tpu as pltpu

EPS = 1e-6
MASK_VALUE = -1e30

D_MODEL = 1024
PLE_DIM = 256
CHUNK = 128
GROUPS = 8
GROUP_DIM = D_MODEL // GROUPS
HEAD_DIM = 64
HEAD_WIDTH = 2 * HEAD_DIM
HEADS = D_MODEL // HEAD_WIDTH
LANES = 128
Q_SCALE = HEAD_DIM ** -0.5 * math.log2(math.e)
COL_U, COL_V, COL_ZA, COL_Q, COL_K, COL_VB, COL_ZB, COL_GA, COL_GB = range(9)
REST_COLS = (COL_U, COL_V, COL_ZA, COL_ZB, COL_GA, COL_GB)

QKV_TILE = 512
LAYER_TILE = 512
ATTN_Q_TILE = 512
ATTN_KV_TILE = 512
ATTN_HEADS_PER_STEP = 4
VMEM_LIMIT_BYTES = 56 * 1024 * 1024

_BF16 = jnp.bfloat16
_F32 = jnp.float32


def _dot(a, b):
    return jnp.dot(a, b, preferred_element_type=_F32)


def _rmsnorm(x, g):
    return x * lax.rsqrt(jnp.mean(x * x, axis=-1, keepdims=True) + EPS) * g


def _gelu(x):
    return 0.5 * x * (1.0 + lax.erf(x * (1.0 / math.sqrt(2.0))))


def _silu(x):
    return x * jax.nn.sigmoid(x)


def _resident(shape, index_map):
    return pl.BlockSpec(shape, index_map, pipeline_mode=pl.Buffered(1))


def _qkv_kernel(x_ref, g_ref, wq_ref, wkt_ref, wv_ref, q_ref, kt_ref, v_ref):
    h = _rmsnorm(x_ref[0], g_ref[...]).astype(_BF16)
    q_ref[0] = (_dot(h, wq_ref[...]) * Q_SCALE).astype(_BF16)
    kt = lax.dot_general(wkt_ref[...], h, (((1,), (1,)), ((), ())),
                         preferred_element_type=_F32)
    kt_ref[0] = kt.astype(_BF16)
    v_ref[0] = _dot(h, wv_ref[...]).astype(_BF16)


def _qkv_projection(x, g, wq, wkt, wv):
    B, S, D = x.shape
    tm = min(QKV_TILE, S)
    tok = pl.BlockSpec((1, tm, D), lambda b, s: (b, s, 0))
    w = _resident((D, D), lambda b, s: (0, 0))
    return pl.pallas_call(
        _qkv_kernel,
        grid=(B, S // tm),
        in_specs=[tok, _resident((1, D), lambda b, s: (0, 0)), w, w, w],
        out_specs=[tok, pl.BlockSpec((1, D, tm), lambda b, s: (b, 0, s)), tok],
        out_shape=[jax.ShapeDtypeStruct((B, S, D), _BF16),
                   jax.ShapeDtypeStruct((B, D, S), _BF16),
                   jax.ShapeDtypeStruct((B, S, D), _BF16)],
        compiler_params=pltpu.CompilerParams(vmem_limit_bytes=VMEM_LIMIT_BYTES),
        name="qkv_projection",
    )(x, g, wq, wkt, wv)


def _attn_kernel(lq_ref, lk_ref, sg_ref, q_ref, kt_ref, v_ref, o_ref,
                 qz_scr, m_scr, l_scr, acc_scr, *, lam_init, tq, tk, heads):
    qi = pl.program_id(2)
    lane = lax.broadcasted_iota(jnp.int32, (tq, HEAD_WIDTH), 1)

    for h in range(heads):
        q = q_ref[0, :, h * HEAD_WIDTH:(h + 1) * HEAD_WIDTH]
        zero = jnp.zeros_like(q)
        qz_scr[h, :tq, :] = jnp.where(lane < HEAD_DIM, q, zero)
        qz_scr[h, tq:, :] = jnp.where(lane >= HEAD_DIM, q, zero)
    m_scr[...] = jnp.full(m_scr.shape, MASK_VALUE, _F32)
    l_scr[...] = jnp.zeros(l_scr.shape, _F32)
    acc_scr[...] = jnp.zeros(acc_scr.shape, _F32)

    def step(j, masked):
        start = pl.multiple_of(j * tk, tk)
        for h in range(heads):
            hs = slice(h * HEAD_WIDTH, (h + 1) * HEAD_WIDTH)
            s = _dot(qz_scr[h], kt_ref[0, hs, pl.ds(start, tk)])
            if masked:
                row = lax.broadcasted_iota(jnp.int32, (2 * tq, tk), 0)
                col = lax.broadcasted_iota(jnp.int32, (2 * tq, tk), 1)
                row = jnp.where(row >= tq, row - tq, row)
                s = jnp.where(col <= row, s, MASK_VALUE)
            tiles = [s[:, t * LANES:(t + 1) * LANES] for t in range(tk // LANES)]
            m_prev = m_scr[h]
            m_new = jnp.maximum(m_prev, jnp.max(functools.reduce(jnp.maximum, tiles),
                                                axis=1, keepdims=True))
            alpha = jnp.exp2(m_prev - m_new)
            ps = [jnp.exp2(t - m_new) for t in tiles]
            l_scr[h] = alpha * l_scr[h] + functools.reduce(jnp.add, ps)
            p = jnp.concatenate(ps, axis=1).astype(_BF16)
            pv = _dot(p, v_ref[0, pl.ds(start, tk), hs])
            acc_scr[h] = alpha * acc_scr[h] + pv
            m_scr[h] = m_new

    def body(j, carry):
        step(j, masked=False)
        return carry

    lax.fori_loop(0, qi, body, 0)
    step(qi, masked=True)

    prod = lq_ref[...] * lk_ref[...]
    e = jnp.exp(jnp.sum(prod, axis=1, keepdims=True))
    lam = e[0:1, :] - e[1:2, :] + lam_init
    for h in range(heads):
        l = jnp.sum(l_scr[h], axis=1, keepdims=True)
        o1 = acc_scr[h, :tq, :] / l[:tq]
        o2 = acc_scr[h, tq:, :] / l[tq:]
        o = o1 - lam * o2
        o = _rmsnorm(o, sg_ref[...]) * (1.0 - lam_init)
        o_ref[0, :, h * HEAD_WIDTH:(h + 1) * HEAD_WIDTH] = o.astype(o_ref.dtype)


def _diff_attention(q, kt, v, lam_q, lam_k, subln_g, lam_init):
    B, S, D = q.shape
    tq = tk = min(ATTN_Q_TILE, S)
    assert ATTN_Q_TILE == ATTN_KV_TILE and S % tq == 0
    heads = ATTN_HEADS_PER_STEP
    width = heads * HEAD_WIDTH
    small = lambda shape: pl.BlockSpec(shape, lambda b, g, i: (0, 0))
    kernel = functools.partial(_attn_kernel, lam_init=lam_init, tq=tq, tk=tk, heads=heads)
    return pl.pallas_call(
        kernel,
        grid=(B, D // width, S // tq),
        in_specs=[small((2, HEAD_DIM)), small((2, HEAD_DIM)), small((1, HEAD_WIDTH)),
                  pl.BlockSpec((1, tq, width), lambda b, g, i: (b, i, g)),
                  pl.BlockSpec((1, width, S), lambda b, g, i: (b, g, 0)),
                  pl.BlockSpec((1, S, width), lambda b, g, i: (b, 0, g))],
        out_specs=pl.BlockSpec((1, tq, width), lambda b, g, i: (b, i, g)),
        out_shape=jax.ShapeDtypeStruct((B, S, D), _BF16),
        scratch_shapes=[pltpu.VMEM((heads, 2 * tq, HEAD_WIDTH), _BF16),
                        pltpu.VMEM((heads, 2 * tq, HEAD_WIDTH), _F32),
                        pltpu.VMEM((heads, 2 * tq, HEAD_WIDTH), _F32),
                        pltpu.VMEM((heads, 2 * tq, HEAD_WIDTH), _F32)],
        compiler_params=pltpu.CompilerParams(vmem_limit_bytes=VMEM_LIMIT_BYTES),
        name="diff_attention",
    )(lam_q, lam_k, subln_g, q, kt, v)


def _layer_kernel(x_ref, o_ref, p_ref, ng_ref, w_ref, lng_ref, lnb_ref, ws_ref, bias_ref,
                  wa_ref, wb_ref, wo_ref, png_ref, wple_ref, wpg_ref, fg_ref, out_ref,
                  *, tm, final):
    x = x_ref[0]
    h = _rmsnorm(x, ng_ref[...]).astype(_BF16)

    def proj(col):
        k = REST_COLS.index(col)
        return _dot(h, w_ref[:, k * D_MODEL:(k + 1) * D_MODEL])

    v = _gelu(proj(COL_V))
    mu = jnp.mean(v, axis=-1, keepdims=True)
    vc = v - mu
    v = vc * lax.rsqrt(jnp.mean(vc * vc, axis=-1, keepdims=True) + EPS) * lng_ref[...] + lnb_ref[...]
    v = v.astype(_BF16)

    n_chunks = tm // CHUNK
    t_out = lax.broadcasted_iota(jnp.int32, (CHUNK, CHUNK), 0)
    t_in = lax.broadcasted_iota(jnp.int32, (CHUNK, CHUNK), 1)
    y_groups = []
    for g in range(GROUPS):
        gs = slice(g * GROUP_DIM, (g + 1) * GROUP_DIM)
        w_g = jnp.where(t_out >= t_in, ws_ref[g], 0.0).astype(_BF16)
        v_wide = jnp.concatenate([v[c * CHUNK:(c + 1) * CHUNK, gs] for c in range(n_chunks)], axis=1)
        y_wide = _dot(w_g, v_wide)
        y_groups.append([y_wide[:, c * GROUP_DIM:(c + 1) * GROUP_DIM] for c in range(n_chunks)])
    y = jnp.concatenate(
        [jnp.concatenate([y_groups[g][c] for g in range(GROUPS)], axis=1) + bias_ref[...]
         for c in range(n_chunks)], axis=0)

    t_a = (_gelu(proj(COL_U)) * y * _silu(proj(COL_ZA))).astype(_BF16)
    y_a = _dot(t_a, wa_ref[...])

    t_b = (o_ref[0].astype(_F32) * _silu(proj(COL_ZB))).astype(_BF16)
    y_b = _dot(t_b, wb_ref[...])

    merged = jax.nn.sigmoid(proj(COL_GA)) * y_a + jax.nn.sigmoid(proj(COL_GB)) * y_b
    x = x + _dot(merged.astype(_BF16), wo_ref[...])

    gate = jax.nn.sigmoid(_dot(_rmsnorm(x, png_ref[...]).astype(_BF16), wpg_ref[...]))
    x = x + _dot(p_ref[0].astype(_BF16), wple_ref[...]) * gate
    if final:
        x = _rmsnorm(x, fg_ref[...])
    out_ref[0] = x


def _fused_layer(x, o, p, ng, w_rest, lng, lnb, ws, bias, wa, wb, wo, png, wple, wpg, fg, final):
    B, S, D = x.shape
    tm = min(LAYER_TILE, S)
    tok = lambda width: pl.BlockSpec((1, tm, width), lambda b, s: (b, s, 0))
    const2 = lambda shape: _resident(shape, lambda b, s: (0, 0))
    row = const2((1, D))
    sq = const2((D, D))
    kernel = functools.partial(_layer_kernel, tm=tm, final=final)
    return pl.pallas_call(
        kernel,
        grid=(B, S // tm),
        in_specs=[tok(D), tok(D), tok(PLE_DIM), row,
                  const2(w_rest.shape), row, row,
                  _resident((GROUPS, CHUNK, CHUNK), lambda b, s: (0, 0, 0)),
                  const2((CHUNK, D)),
                  sq, sq, sq, row, const2((PLE_DIM, D)), sq, row],
        out_specs=tok(D),
        out_shape=jax.ShapeDtypeStruct((B, S, D), _F32),
        compiler_params=pltpu.CompilerParams(vmem_limit_bytes=VMEM_LIMIT_BYTES),
        name="fused_layer",
    )(x, o, p, ng, w_rest, lng, lnb, ws, bias, wa, wb, wo, png, wple, wpg, fg)


def _lambda_init(layer_idx):
    return 0.8 - 0.6 * math.exp(-0.3 * layer_idx)


def kernel(x, p, norm_g, w_in, a_ln_g, a_ln_b, a_ws, a_bs, lam_q, lam_k, subln_g,
           w_a_out, w_b_out, w_o, ple_norm_g, w_ple, w_ple_gate, final_g):
    depth = w_in.shape[0]
    D = D_MODEL
    col = lambda w, c: w[:, c * D:(c + 1) * D]
    row = lambda g: g.reshape(1, -1)
    for i in range(depth):
        w = w_in[i].astype(_BF16)
        wq, wv = col(w, COL_Q), col(w, COL_VB)
        wkt = col(w, COL_K).T
        w_rest = jnp.concatenate([col(w, c) for c in REST_COLS], axis=1)
        bias = jnp.repeat(a_bs[i].T, GROUP_DIM, axis=1)
        q, kt, v = _qkv_projection(x, row(norm_g[i]), wq, wkt, wv)
        o = _diff_attention(q, kt, v, lam_q[i], lam_k[i], row(subln_g[i]), _lambda_init(i))
        x = _fused_layer(
            x, o, p[i], row(norm_g[i]), w_rest, row(a_ln_g[i]), row(a_ln_b[i]), a_ws[i], bias,
            w_a_out[i].astype(_BF16), w_b_out[i].astype(_BF16), w_o[i].astype(_BF16),
            row(ple_norm_g[i]), w_ple[i].astype(_BF16), w_ple_gate[i].astype(_BF16),
            row(final_g), final=(i == depth - 1))
    return x
```

```python
import functools
import math

import jax
import jax.numpy as jnp
from jax import lax
from jax.experimental import pallas as pl
from jax.experimental.pallas import tpu as pltpu

EPS = 1e-6
MASK_VALUE = -1e30

D_MODEL = 1024
PLE_DIM = 256
CHUNK = 128
GROUPS = 8
GROUP_DIM = D_MODEL // GROUPS
HEAD_DIM = 64
HEAD_WIDTH = 2 * HEAD_DIM
HEADS = D_MODEL // HEAD_WIDTH
LANES = 128
Q_SCALE = HEAD_DIM ** -0.5 * math.log2(math.e)
COL_U, COL_V, COL_ZA, COL_Q, COL_K, COL_VB, COL_ZB, COL_GA, COL_GB = range(9)
REST_COLS = (COL_U, COL_V, COL_ZA, COL_ZB, COL_GA, COL_GB)

QKV_TILE = 512
LAYER_TILE = 512
ATTN_TILE = 512
ATTN_HEADS_PER_STEP = 4
VMEM_LIMIT_BYTES = 56 * 1024 * 1024

_BF16 = jnp.bfloat16
_F32 = jnp.float32


def _dot(a, b):
    return jnp.dot(a, b, preferred_element_type=_F32)


def _rmsnorm(x, g):
    return x * lax.rsqrt(jnp.mean(x * x, axis=-1, keepdims=True) + EPS) * g


def _gelu(x):
    return 0.5 * x * (1.0 + lax.erf(x * (1.0 / math.sqrt(2.0))))


def _silu(x):
    return x * jax.nn.sigmoid(x)


def _resident(shape, index_map):
    return pl.BlockSpec(shape, index_map, pipeline_mode=pl.Buffered(1))


def _qkv_kernel(x_ref, g_ref, w_ref, q_ref, kt_ref, v_ref):
    D = D_MODEL
    h = _rmsnorm(x_ref[0], g_ref[...]).astype(_BF16)
    q_ref[0] = (_dot(h, w_ref[:, :D]) * Q_SCALE).astype(_BF16)
    kt_ref[0] = _dot(h, w_ref[:, D:2 * D]).T.astype(_BF16)
    v_ref[0] = _dot(h, w_ref[:, 2 * D:]).astype(_BF16)


def _layer_block(shape, layer, *tail):
    return _resident((None, *shape), lambda b, s: (layer, *tail))


def _qkv_projection(x, g, w_in, layer):
    B, S, D = x.shape
    assert (COL_Q, COL_K, COL_VB) == (3, 4, 5)
    tm = min(QKV_TILE, S)
    tok = pl.BlockSpec((1, tm, D), lambda b, s: (b, s, 0))
    return pl.pallas_call(
        _qkv_kernel,
        grid=(B, S // tm),
        in_specs=[tok, _layer_block((1, D), layer, 0, 0), _layer_block((D, 3 * D), layer, 0, 1)],
        out_specs=[tok, pl.BlockSpec((1, D, tm), lambda b, s: (b, 0, s)), tok],
        out_shape=[jax.ShapeDtypeStruct((B, S, D), _BF16),
                   jax.ShapeDtypeStruct((B, D, S), _BF16),
                   jax.ShapeDtypeStruct((B, S, D), _BF16)],
        compiler_params=pltpu.CompilerParams(vmem_limit_bytes=VMEM_LIMIT_BYTES),
        name="qkv_projection",
    )(x, g, w_in)


def _attn_kernel(lq_ref, lk_ref, sg_ref, q_ref, kt_ref, v_ref, o_ref,
                 qz_scr, s_scr, m_scr, l_scr, acc_scr, *, lam_init, tq, heads, n_q):
    qi = pl.program_id(2)
    tk = tq
    n_blk = tk // LANES
    lane = lax.broadcasted_iota(jnp.int32, (tq, HEAD_WIDTH), 1)

    def head_lanes(h):
        return slice(h * HEAD_WIDTH, (h + 1) * HEAD_WIDTH)

    def scores(h, j):
        start = pl.multiple_of(j * tk, tk)
        s_scr[h] = _dot(qz_scr[h], kt_ref[0, head_lanes(h), pl.ds(start, tk)])

    def softmax_rows(blocks, n_zero, m_prev):
        m_cur = jnp.max(functools.reduce(jnp.maximum, blocks), axis=1, keepdims=True)
        if m_prev is None:
            m_new, alpha = jnp.broadcast_to(m_cur, blocks[0].shape), None
        else:
            m_new = jnp.maximum(m_prev, m_cur)
            alpha = jnp.exp2(m_prev - m_new)
        p = [jnp.exp2(b - m_new).astype(_BF16) for b in blocks]
        p += [jnp.zeros(blocks[0].shape, _BF16)] * n_zero
        return m_new, alpha, jnp.concatenate(p, axis=1)

    def weighted_values(h, j, p):
        start = pl.multiple_of(j * tk, tk)
        v = v_ref[0, pl.ds(start, tk), head_lanes(h)]
        pv = _dot(p, jnp.concatenate([v, jnp.ones_like(v)], axis=1))
        return pv[:, :HEAD_WIDTH], pv[:, HEAD_WIDTH:]

    def update(h, j, first):
        s = s_scr[h]
        blocks = [s[:, t * LANES:(t + 1) * LANES] for t in range(n_blk)]
        m_new, alpha, p = softmax_rows(blocks, 0, None if first else m_scr[h])
        pv, row_sum = weighted_values(h, j, p)
        acc_scr[h] = pv if first else alpha * acc_scr[h] + pv
        l_scr[h] = row_sum if first else alpha * l_scr[h] + row_sum
        m_scr[h] = m_new

    def finish(h):
        row = lax.broadcasted_iota(jnp.int32, (LANES, LANES), 0)
        col = lax.broadcasted_iota(jnp.int32, (LANES, LANES), 1)
        stats = []
        for r in range(2 * n_blk):
            i = r % n_blk
            rows = slice(r * LANES, (r + 1) * LANES)
            blocks = [s_scr[h, rows, t * LANES:(t + 1) * LANES] for t in range(i + 1)]
            blocks[i] = jnp.where(col <= row, blocks[i], MASK_VALUE)
            _, alpha, p = softmax_rows(blocks, n_blk - 1 - i, m_scr[h, rows, :])
            stats.append((alpha, p))
        alpha, p = (jnp.concatenate(part, axis=0) for part in zip(*stats))
        pv, row_sum = weighted_values(h, qi, p)
        o = (alpha * acc_scr[h] + pv) / (alpha * l_scr[h] + row_sum)
        o = _rmsnorm(o[:tq] - lam * o[tq:], sg_ref[...]) * (1.0 - lam_init)
        o_ref[0, :, head_lanes(h)] = o.astype(o_ref.dtype)

    def stacked_q(t):
        rows = pl.ds(pl.multiple_of(t * tq, tq), tq)
        for h in range(heads):
            q = q_ref[0, rows, head_lanes(h)]
            zero = jnp.zeros_like(q)
            qz_scr[h, :tq, :] = jnp.where(lane < HEAD_DIM, q, zero)
            qz_scr[h, tq:, :] = jnp.where(lane >= HEAD_DIM, q, zero)

    early, late = range(heads // 2), range(heads // 2, heads)

    def sweep(j, first):
        for h in early:
            scores(h, j)
        for h in late:
            update(h, j, first)
            scores(h, j + 1)
        for h in early:
            update(h, j, first)

    @pl.when(qi == 0)
    def _():
        stacked_q(0)
        m_scr[...] = jnp.full(m_scr.shape, MASK_VALUE, _F32)
        l_scr[...] = jnp.zeros(l_scr.shape, _F32)
        acc_scr[...] = jnp.zeros(acc_scr.shape, _F32)
        for h in late:
            scores(h, 0)

    @pl.when(qi > 0)
    def _():
        sweep(0, first=True)

    def body(j, carry):
        sweep(j, first=False)
        return carry

    lax.fori_loop(1, qi, body, 0)

    prod = lq_ref[...] * lk_ref[...]
    e = jnp.exp(jnp.sum(prod, axis=1, keepdims=True))
    lam = e[0:1, :] - e[1:2, :] + lam_init
    for h in early:
        scores(h, qi)
    for h in (*late, *early):
        finish(h)

    stacked_q(jnp.minimum(qi + 1, n_q - 1))
    for h in late:
        scores(h, 0)


def _diff_attention(q, kt, v, lam_q, lam_k, subln_g, lam_init):
    B, S, D = q.shape
    tq = min(ATTN_TILE, S)
    assert S % tq == 0
    heads = ATTN_HEADS_PER_STEP
    width = heads * HEAD_WIDTH
    small = lambda shape: pl.BlockSpec(shape, lambda b, g, i: (0, 0))
    kernel = functools.partial(_attn_kernel, lam_init=lam_init, tq=tq, heads=heads, n_q=S // tq)
    stat = pltpu.VMEM((heads, 2 * tq, HEAD_WIDTH), _F32)
    score = pltpu.VMEM((heads, 2 * tq, tq), _F32)
    seq = pl.BlockSpec((1, S, width), lambda b, g, i: (b, 0, g))
    return pl.pallas_call(
        kernel,
        grid=(B, D // width, S // tq),
        in_specs=[small((2, HEAD_DIM)), small((2, HEAD_DIM)), small((1, HEAD_WIDTH)),
                  seq, pl.BlockSpec((1, width, S), lambda b, g, i: (b, g, 0)), seq],
        out_specs=pl.BlockSpec((1, tq, width), lambda b, g, i: (b, i, g)),
        out_shape=jax.ShapeDtypeStruct((B, S, D), _BF16),
        scratch_shapes=[pltpu.VMEM((heads, 2 * tq, HEAD_WIDTH), _BF16),
                        score, stat, stat, stat],
        compiler_params=pltpu.CompilerParams(
            dimension_semantics=("arbitrary", "arbitrary", "arbitrary"),
            vmem_limit_bytes=VMEM_LIMIT_BYTES),
        name="diff_attention",
    )(lam_q, lam_k, subln_g, q, kt, v)


def _layer_kernel(x_ref, o_ref, p_ref, ng_ref, w_lo_ref, w_hi_ref, lng_ref, lnb_ref, ws_ref, bias_ref,
                  wa_ref, wb_ref, wo_ref, png_ref, wple_ref, wpg_ref, fg_ref, out_ref,
                  *, tm, final):
    x = x_ref[0]
    h = _rmsnorm(x, ng_ref[...]).astype(_BF16)

    def proj(col):
        w_ref, k = (w_lo_ref, col) if col < 3 else (w_hi_ref, col - 6)
        return _dot(h, w_ref[:, k * D_MODEL:(k + 1) * D_MODEL])

    v = _gelu(proj(COL_V))
    mu = jnp.mean(v, axis=-1, keepdims=True)
    vc = v - mu
    v = vc * lax.rsqrt(jnp.mean(vc * vc, axis=-1, keepdims=True) + EPS) * lng_ref[...] + lnb_ref[...]
    v = v.astype(_BF16)

    n_chunks = tm // CHUNK
    t_out = lax.broadcasted_iota(jnp.int32, (CHUNK, CHUNK), 0)
    t_in = lax.broadcasted_iota(jnp.int32, (CHUNK, CHUNK), 1)
    y_groups = []
    for g in range(GROUPS):
        gs = slice(g * GROUP_DIM, (g + 1) * GROUP_DIM)
        w_g = jnp.where(t_out >= t_in, ws_ref[g], 0.0).astype(_BF16)
        v_wide = jnp.concatenate([v[c * CHUNK:(c + 1) * CHUNK, gs] for c in range(n_chunks)], axis=1)
        y_wide = _dot(w_g, v_wide)
        y_groups.append([y_wide[:, c * GROUP_DIM:(c + 1) * GROUP_DIM] for c in range(n_chunks)])
    y = jnp.concatenate(
        [jnp.concatenate([y_groups[g][c] for g in range(GROUPS)], axis=1) + bias_ref[...]
         for c in range(n_chunks)], axis=0)

    t_a = (_gelu(proj(COL_U)) * y * _silu(proj(COL_ZA))).astype(_BF16)
    y_a = _dot(t_a, wa_ref[...])

    t_b = (o_ref[0].astype(_F32) * _silu(proj(COL_ZB))).astype(_BF16)
    y_b = _dot(t_b, wb_ref[...])

    merged = jax.nn.sigmoid(proj(COL_GA)) * y_a + jax.nn.sigmoid(proj(COL_GB)) * y_b
    x = x + _dot(merged.astype(_BF16), wo_ref[...])

    gate = jax.nn.sigmoid(_dot(_rmsnorm(x, png_ref[...]).astype(_BF16), wpg_ref[...]))
    x = x + _dot(p_ref[0].astype(_BF16), wple_ref[...]) * gate
    if final:
        x = _rmsnorm(x, fg_ref[...])
    out_ref[0] = x


def _fused_layer(x, o, p, ng, w_in, lng, lnb, ws, bias, wa, wb, wo, png, wple, wpg, fg, layer, final):
    B, S, D = x.shape
    assert REST_COLS == (0, 1, 2, 6, 7, 8)
    tm = min(LAYER_TILE, S)
    tok = lambda width: pl.BlockSpec((1, tm, width), lambda b, s: (b, s, 0))
    row = _layer_block((1, D), layer, 0, 0)
    sq = _layer_block((D, D), layer, 0, 0)
    kernel = functools.partial(_layer_kernel, tm=tm, final=final)
    return pl.pallas_call(
        kernel,
        grid=(B, S // tm),
        in_specs=[tok(D), tok(D),
                  pl.BlockSpec((None, 1, tm, PLE_DIM), lambda b, s: (layer, b, s, 0)),
                  row,
                  _layer_block((D, 3 * D), layer, 0, 0), _layer_block((D, 3 * D), layer, 0, 2),
                  row, row,
                  _layer_block((GROUPS, CHUNK, CHUNK), layer, 0, 0, 0),
                  _layer_block((CHUNK, D), layer, 0, 0),
                  sq, sq, sq, row, _layer_block((PLE_DIM, D), layer, 0, 0), sq,
                  _resident((1, D), lambda b, s: (0, 0))],
        out_specs=tok(D),
        out_shape=jax.ShapeDtypeStruct((B, S, D), _F32),
        compiler_params=pltpu.CompilerParams(vmem_limit_bytes=VMEM_LIMIT_BYTES),
        name="fused_layer",
    )(x, o, p, ng, w_in, w_in, lng, lnb, ws, bias, wa, wb, wo, png, wple, wpg, fg)


def _lambda_init(layer_idx):
    return 0.8 - 0.6 * math.exp(-0.3 * layer_idx)


def kernel(x, p, norm_g, w_in, a_ln_g, a_ln_b, a_ws, a_bs, lam_q, lam_k, subln_g,
           w_a_out, w_b_out, w_o, ple_norm_g, w_ple, w_ple_gate, final_g):
    depth = w_in.shape[0]
    rows = lambda g: g[:, None, :]
    bf16 = lambda w: w.astype(_BF16)
    w_in, w_a_out, w_b_out, w_o, w_ple, w_ple_gate = map(
        bf16, (w_in, w_a_out, w_b_out, w_o, w_ple, w_ple_gate))
    norm_g, a_ln_g, a_ln_b, ple_norm_g = map(rows, (norm_g, a_ln_g, a_ln_b, ple_norm_g))
    bias = jnp.repeat(jnp.swapaxes(a_bs, 1, 2), GROUP_DIM, axis=2)
    for i in range(depth):
        q, kt, v = _qkv_projection(x, norm_g, w_in, i)
        o = _diff_attention(q, kt, v, lam_q[i], lam_k[i], subln_g[i][None, :], _lambda_init(i))
        x = _fused_layer(x, o, p, norm_g, w_in, a_ln_g, a_ln_b, a_ws, bias, w_a_out, w_b_out, w_o,
                         ple_norm_g, w_ple, w_ple_gate, final_g[None, :], i, final=(i == depth - 1))
    return x
```

```python
import functools
import math

import jax
import jax.numpy as jnp
from jax import lax
from jax.experimental import pallas as pl
from jax.experimental.pallas import tpu as pltpu

EPS = 1e-6
MASK_VALUE = -1e30

D_MODEL = 1024
PLE_DIM = 256
CHUNK = 128
GROUPS = 8
GROUP_DIM = D_MODEL // GROUPS
HEAD_DIM = 64
HEAD_WIDTH = 2 * HEAD_DIM
HEADS = D_MODEL // HEAD_WIDTH
LANES = 128
Q_SCALE = HEAD_DIM ** -0.5 * math.log2(math.e)
COL_U, COL_V, COL_ZA, COL_Q, COL_K, COL_VB, COL_ZB, COL_GA, COL_GB = range(9)
REST_COLS = (COL_U, COL_V, COL_ZA, COL_ZB, COL_GA, COL_GB)

QKV_TILE = 512
LAYER_TILE = 512
ATTN_TILE = 512
ATTN_HEADS_PER_STEP = 4
VMEM_LIMIT_BYTES = 56 * 1024 * 1024

_BF16 = jnp.bfloat16
_F32 = jnp.float32


def _dot(a, b):
    return jnp.dot(a, b, preferred_element_type=_F32)


def _rmsnorm(x, g):
    return x * lax.rsqrt(jnp.mean(x * x, axis=-1, keepdims=True) + EPS) * g


def _gelu(x):
    return 0.5 * x * (1.0 + lax.erf(x * (1.0 / math.sqrt(2.0))))


def _silu(x):
    return x * jax.nn.sigmoid(x)


def _resident(shape, index_map):
    return pl.BlockSpec(shape, index_map, pipeline_mode=pl.Buffered(1))


def _qkv_kernel(x_ref, g_ref, w_ref, q_ref, kt_ref, v_ref):
    D = D_MODEL
    h = _rmsnorm(x_ref[0], g_ref[...]).astype(_BF16)
    q_ref[0] = (_dot(h, w_ref[:, :D]) * Q_SCALE).astype(_BF16)
    kt_ref[0] = _dot(h, w_ref[:, D:2 * D]).T.astype(_BF16)
    v_ref[0] = _dot(h, w_ref[:, 2 * D:]).astype(_BF16)


def _layer_block(shape, layer, *tail):
    return _resident((None, *shape), lambda b, s: (layer, *tail))


def _qkv_projection(x, g, w_in, layer):
    B, S, D = x.shape
    assert (COL_Q, COL_K, COL_VB) == (3, 4, 5)
    tm = min(QKV_TILE, S)
    tok = pl.BlockSpec((1, tm, D), lambda b, s: (b, s, 0))
    return pl.pallas_call(
        _qkv_kernel,
        grid=(B, S // tm),
        in_specs=[tok, _layer_block((1, D), layer, 0, 0), _layer_block((D, 3 * D), layer, 0, 1)],
        out_specs=[tok, pl.BlockSpec((1, D, tm), lambda b, s: (b, 0, s)), tok],
        out_shape=[jax.ShapeDtypeStruct((B, S, D), _BF16),
                   jax.ShapeDtypeStruct((B, D, S), _BF16),
                   jax.ShapeDtypeStruct((B, S, D), _BF16)],
        compiler_params=pltpu.CompilerParams(vmem_limit_bytes=VMEM_LIMIT_BYTES),
        name="qkv_projection",
    )(x, g, w_in)


def _attn_kernel(lq_ref, lk_ref, sg_ref, q_ref, kt_ref, v_ref, o_ref,
                 qz_scr, s_scr, m_scr, l_scr, acc_scr, *, lam_init, tq, heads, n_q):
    qi = pl.program_id(2)
    tk = tq
    n_blk = tk // LANES
    lane = lax.broadcasted_iota(jnp.int32, (tq, HEAD_WIDTH), 1)

    def head_lanes(h):
        return slice(h * HEAD_WIDTH, (h + 1) * HEAD_WIDTH)

    def scores(h, j):
        start = pl.multiple_of(j * tk, tk)
        s_scr[h] = _dot(qz_scr[h], kt_ref[0, head_lanes(h), pl.ds(start, tk)])

    def softmax_rows(blocks, n_zero, m_prev):
        m_cur = jnp.max(functools.reduce(jnp.maximum, blocks), axis=1, keepdims=True)
        if m_prev is None:
            m_new, alpha = jnp.broadcast_to(m_cur, blocks[0].shape), None
        else:
            m_new = jnp.maximum(m_prev, m_cur)
            alpha = jnp.exp2(m_prev - m_new)
        p = [jnp.exp2(b - m_new).astype(_BF16) for b in blocks]
        p += [jnp.zeros(blocks[0].shape, _BF16)] * n_zero
        return m_new, alpha, jnp.concatenate(p, axis=1)

    def weighted_values(h, j, p):
        start = pl.multiple_of(j * tk, tk)
        v = v_ref[0, pl.ds(start, p.shape[1]), head_lanes(h)]
        pv = _dot(p, jnp.concatenate([v, jnp.ones_like(v)], axis=1))
        return pv[:, :HEAD_WIDTH], pv[:, HEAD_WIDTH:]

    def update(h, j, first):
        s = s_scr[h]
        blocks = [s[:, t * LANES:(t + 1) * LANES] for t in range(n_blk)]
        m_new, alpha, p = softmax_rows(blocks, 0, None if first else m_scr[h])
        pv, row_sum = weighted_values(h, j, p)
        acc_scr[h] = pv if first else alpha * acc_scr[h] + pv
        l_scr[h] = row_sum if first else alpha * l_scr[h] + row_sum
        m_scr[h] = m_new

    def finish(h):
        row = lax.broadcasted_iota(jnp.int32, (LANES, LANES), 0)
        col = lax.broadcasted_iota(jnp.int32, (LANES, LANES), 1)
        spans = []
        for span in range(n_blk):
            n_keys = 2 * (span % (n_blk // 2) + 1)
            stats = []
            for r in (2 * span, 2 * span + 1):
                i = r % n_blk
                rows = slice(r * LANES, (r + 1) * LANES)
                blocks = [s_scr[h, rows, t * LANES:(t + 1) * LANES] for t in range(i + 1)]
                blocks[i] = jnp.where(col <= row, blocks[i], MASK_VALUE)
                _, alpha, p = softmax_rows(blocks, n_keys - 1 - i, m_scr[h, rows, :])
                stats.append((alpha, p))
            alpha, p = (jnp.concatenate(part, axis=0) for part in zip(*stats))
            rows = slice(2 * span * LANES, 2 * (span + 1) * LANES)
            pv, row_sum = weighted_values(h, qi, p)
            spans.append((alpha * acc_scr[h, rows, :] + pv) / (alpha * l_scr[h, rows, :] + row_sum))
        o = jnp.concatenate(spans, axis=0)
        o = _rmsnorm(o[:tq] - lam * o[tq:], sg_ref[...]) * (1.0 - lam_init)
        o_ref[0, :, head_lanes(h)] = o.astype(o_ref.dtype)

    def stacked_q(t):
        rows = pl.ds(pl.multiple_of(t * tq, tq), tq)
        for h in range(heads):
            q = q_ref[0, rows, head_lanes(h)]
            zero = jnp.zeros_like(q)
            qz_scr[h, :tq, :] = jnp.where(lane < HEAD_DIM, q, zero)
            qz_scr[h, tq:, :] = jnp.where(lane >= HEAD_DIM, q, zero)

    early, late = range(heads // 2), range(heads // 2, heads)

    def sweep(j, first):
        for h in early:
            scores(h, j)
        for h in late:
            update(h, j, first)
            scores(h, j + 1)
        for h in early:
            update(h, j, first)

    @pl.when(qi == 0)
    def _():
        stacked_q(0)
        m_scr[...] = jnp.full(m_scr.shape, MASK_VALUE, _F32)
        l_scr[...] = jnp.zeros(l_scr.shape, _F32)
        acc_scr[...] = jnp.zeros(acc_scr.shape, _F32)
        for h in late:
            scores(h, 0)

    def body(j, carry):
        sweep(j, first=False)
        return carry

    lax.fori_loop(1, qi, body, 0)

    prod = lq_ref[...] * lk_ref[...]
    e = jnp.exp(jnp.sum(prod, axis=1, keepdims=True))
    lam = e[0:1, :] - e[1:2, :] + lam_init

    def diagonal(start_next):
        for h in early:
            scores(h, qi)
        if start_next:
            stacked_q(qi + 1)
        for h in late:
            finish(h)
            if start_next:
                scores(h, 0)
        for h in early:
            finish(h)
        if start_next:
            sweep(0, first=True)

    @pl.when(qi < n_q - 1)
    def _():
        diagonal(start_next=True)

    @pl.when(qi == n_q - 1)
    def _():
        diagonal(start_next=False)


def _diff_attention(q, kt, v, lam_q, lam_k, subln_g, lam_init):
    B, S, D = q.shape
    tq = min(ATTN_TILE, S)
    assert S % tq == 0 and tq % (2 * LANES) == 0
    heads = ATTN_HEADS_PER_STEP
    width = heads * HEAD_WIDTH
    small = lambda shape: pl.BlockSpec(shape, lambda b, g, i: (0, 0))
    kernel = functools.partial(_attn_kernel, lam_init=lam_init, tq=tq, heads=heads, n_q=S // tq)
    stat = pltpu.VMEM((heads, 2 * tq, HEAD_WIDTH), _F32)
    score = pltpu.VMEM((heads, 2 * tq, tq), _F32)
    seq = pl.BlockSpec((1, S, width), lambda b, g, i: (b, 0, g))
    return pl.pallas_call(
        kernel,
        grid=(B, D // width, S // tq),
        in_specs=[small((2, HEAD_DIM)), small((2, HEAD_DIM)), small((1, HEAD_WIDTH)),
                  seq, pl.BlockSpec((1, width, S), lambda b, g, i: (b, g, 0)), seq],
        out_specs=pl.BlockSpec((1, tq, width), lambda b, g, i: (b, i, g)),
        out_shape=jax.ShapeDtypeStruct((B, S, D), _BF16),
        scratch_shapes=[pltpu.VMEM((heads, 2 * tq, HEAD_WIDTH), _BF16),
                        score, stat, stat, stat],
        compiler_params=pltpu.CompilerParams(
            dimension_semantics=("arbitrary", "arbitrary", "arbitrary"),
            vmem_limit_bytes=VMEM_LIMIT_BYTES),
        name="diff_attention",
    )(lam_q, lam_k, subln_g, q, kt, v)


def _layer_kernel(x_ref, o_ref, p_ref, ng_ref, w_lo_ref, w_hi_ref, lng_ref, lnb_ref, ws_ref, bias_ref,
                  wa_ref, wb_ref, wo_ref, png_ref, wple_ref, wpg_ref, fg_ref, out_ref,
                  *, tm, final):
    x = x_ref[0]
    h = _rmsnorm(x, ng_ref[...]).astype(_BF16)

    def proj(col):
        w_ref, k = (w_lo_ref, col) if col < 3 else (w_hi_ref, col - 6)
        return _dot(h, w_ref[:, k * D_MODEL:(k + 1) * D_MODEL])

    v = _gelu(proj(COL_V))
    mu = jnp.mean(v, axis=-1, keepdims=True)
    vc = v - mu
    v = vc * lax.rsqrt(jnp.mean(vc * vc, axis=-1, keepdims=True) + EPS) * lng_ref[...] + lnb_ref[...]
    v = v.astype(_BF16)

    n_chunks = tm // CHUNK
    t_out = lax.broadcasted_iota(jnp.int32, (CHUNK, CHUNK), 0)
    t_in = lax.broadcasted_iota(jnp.int32, (CHUNK, CHUNK), 1)
    y_groups = []
    for g in range(GROUPS):
        gs = slice(g * GROUP_DIM, (g + 1) * GROUP_DIM)
        w_g = jnp.where(t_out >= t_in, ws_ref[g], 0.0).astype(_BF16)
        v_wide = jnp.concatenate([v[c * CHUNK:(c + 1) * CHUNK, gs] for c in range(n_chunks)], axis=1)
        y_wide = _dot(w_g, v_wide)
        y_groups.append([y_wide[:, c * GROUP_DIM:(c + 1) * GROUP_DIM] for c in range(n_chunks)])
    y = jnp.concatenate(
        [jnp.concatenate([y_groups[g][c] for g in range(GROUPS)], axis=1) + bias_ref[...]
         for c in range(n_chunks)], axis=0)

    t_a = (_gelu(proj(COL_U)) * y * _silu(proj(COL_ZA))).astype(_BF16)
    y_a = _dot(t_a, wa_ref[...])

    t_b = (o_ref[0].astype(_F32) * _silu(proj(COL_ZB))).astype(_BF16)
    y_b = _dot(t_b, wb_ref[...])

    merged = jax.nn.sigmoid(proj(COL_GA)) * y_a + jax.nn.sigmoid(proj(COL_GB)) * y_b
    x = x + _dot(merged.astype(_BF16), wo_ref[...])

    gate = jax.nn.sigmoid(_dot(_rmsnorm(x, png_ref[...]).astype(_BF16), wpg_ref[...]))
    x = x + _dot(p_ref[0].astype(_BF16), wple_ref[...]) * gate
    if final:
        x = _rmsnorm(x, fg_ref[...])
    out_ref[0] = x


def _fused_layer(x, o, p, ng, w_in, lng, lnb, ws, bias, wa, wb, wo, png, wple, wpg, fg, layer, final):
    B, S, D = x.shape
    assert REST_COLS == (0, 1, 2, 6, 7, 8)
    tm = min(LAYER_TILE, S)
    tok = lambda width: pl.BlockSpec((1, tm, width), lambda b, s: (b, s, 0))
    row = _layer_block((1, D), layer, 0, 0)
    sq = _layer_block((D, D), layer, 0, 0)
    kernel = functools.partial(_layer_kernel, tm=tm, final=final)
    return pl.pallas_call(
        kernel,
        grid=(B, S // tm),
        in_specs=[tok(D), tok(D),
                  pl.BlockSpec((None, 1, tm, PLE_DIM), lambda b, s: (layer, b, s, 0)),
                  row,
                  _layer_block((D, 3 * D), layer, 0, 0), _layer_block((D, 3 * D), layer, 0, 2),
                  row, row,
                  _layer_block((GROUPS, CHUNK, CHUNK), layer, 0, 0, 0),
                  _layer_block((CHUNK, D), layer, 0, 0),
                  sq, sq, sq, row, _layer_block((PLE_DIM, D), layer, 0, 0), sq,
                  _resident((1, D), lambda b, s: (0, 0))],
        out_specs=tok(D),
        out_shape=jax.ShapeDtypeStruct((B, S, D), _F32),
        compiler_params=pltpu.CompilerParams(vmem_limit_bytes=VMEM_LIMIT_BYTES),
        name="fused_layer",
    )(x, o, p, ng, w_in, w_in, lng, lnb, ws, bias, wa, wb, wo, png, wple, wpg, fg)


def _lambda_init(layer_idx):
    return 0.8 - 0.6 * math.exp(-0.3 * layer_idx)


def kernel(x, p, norm_g, w_in, a_ln_g, a_ln_b, a_ws, a_bs, lam_q, lam_k, subln_g,
           w_a_out, w_b_out, w_o, ple_norm_g, w_ple, w_ple_gate, final_g):
    depth = w_in.shape[0]
    rows = lambda g: g[:, None, :]
    bf16 = lambda w: w.astype(_BF16)
    w_in, w_a_out, w_b_out, w_o, w_ple, w_ple_gate = map(
        bf16, (w_in, w_a_out, w_b_out, w_o, w_ple, w_ple_gate))
    norm_g, a_ln_g, a_ln_b, ple_norm_g = map(rows, (norm_g, a_ln_g, a_ln_b, ple_norm_g))
    bias = jnp.repeat(jnp.swapaxes(a_bs, 1, 2), GROUP_DIM, axis=2)
    for i in range(depth):
        q, kt, v = _qkv_projection(x, norm_g, w_in, i)
        o = _diff_attention(q, kt, v, lam_q[i], lam_k[i], subln_g[i][None, :], _lambda_init(i))
        x = _fused_layer(x, o, p, norm_g, w_in, a_ln_g, a_ln_b, a_ws, bias, w_a_out, w_b_out, w_o,
                         ple_norm_g, w_ple, w_ple_gate, final_g[None, :], i, final=(i == depth - 1))
    return x
```
